```python
import math
import jax, jax.numpy as jnp
from jax import lax
import numpy as np

D_MODEL = 2048
BATCH = 8
SEQ = 2048
DEPTH = 1

MLA_HEADS = 8
QK_NOPE_DIM = 128
QK_ROPE_DIM = 64
QK_HEAD_DIM = QK_NOPE_DIM + QK_ROPE_DIM
V_HEAD_DIM = 128
Q_LORA_RANK = 512
KV_LORA_RANK = 256
ROPE_THETA = 10000.0
ATTN_BLOCK = 128
MLA_WIDTH = MLA_HEADS * V_HEAD_DIM

SSD_HEADS = 16
SSD_HEAD_DIM = 64
SSD_D_INNER = SSD_HEADS * SSD_HEAD_DIM
SSD_GROUPS = 2
SSD_STATE = 128
SSD_CONV = 4
SSD_CHUNK = 128
SSD_CONV_DIM = SSD_D_INNER + 2 * SSD_GROUPS * SSD_STATE

MIX_WIDTH = MLA_WIDTH + SSD_D_INNER

IN_SPLITS = (
    Q_LORA_RANK,
    Q_LORA_RANK + KV_LORA_RANK,
    Q_LORA_RANK + KV_LORA_RANK + QK_ROPE_DIM,
    Q_LORA_RANK + KV_LORA_RANK + QK_ROPE_DIM + SSD_D_INNER,
    Q_LORA_RANK + KV_LORA_RANK + QK_ROPE_DIM + SSD_D_INNER + SSD_CONV_DIM,
)
IN_COLS = Q_LORA_RANK + KV_LORA_RANK + QK_ROPE_DIM + SSD_D_INNER + SSD_CONV_DIM + SSD_HEADS

PEER_HEADS = 8
PEER_KEYS = 128
PEER_EXPERTS = PEER_KEYS * PEER_KEYS
PEER_TOPK = 16
PEER_KEY_DIM = 256
PEER_HALF = PEER_KEY_DIM // 2
PEER_TOKEN_BLOCK = 128

N_MOD = 6
EPS = 1e-6

kernel_name = "hybrid_mla_ssd_peer_adaln_block"


def rms_norm(x, gain=None):
    xf = x.astype(jnp.float32)
    y = xf * lax.rsqrt(jnp.mean(xf * xf, axis=-1, keepdims=True) + EPS)
    if gain is not None:
        y = y * gain.astype(jnp.float32)
    return y.astype(x.dtype)


def rope(x, positions):
    r = x.shape[-1]
    half = r // 2
    inv_freq = 1.0 / (ROPE_THETA ** (jnp.arange(half, dtype=jnp.float32) * (2.0 / r)))
    ang = positions.astype(jnp.float32)[:, :, None, None] * inv_freq
    cos, sin = jnp.cos(ang), jnp.sin(ang)
    xf = x.astype(jnp.float32)
    x1, x2 = xf[..., :half], xf[..., half:]
    return jnp.concatenate([x1 * cos - x2 * sin, x2 * cos + x1 * sin], axis=-1).astype(x.dtype)


def mla_attention(c_q, c_kv, k_rope, positions, q_a_norm, w_uq, kv_a_norm, w_ukv, q_norm, k_norm):
    b, s, _ = c_q.shape
    q = (rms_norm(c_q, q_a_norm) @ w_uq).reshape(b, s, MLA_HEADS, QK_HEAD_DIM)
    kv = (rms_norm(c_kv, kv_a_norm) @ w_ukv).reshape(b, s, MLA_HEADS, QK_NOPE_DIM + V_HEAD_DIM)
    k_nope, v = kv[..., :QK_NOPE_DIM], kv[..., QK_NOPE_DIM:]
    k_pe = jnp.broadcast_to(k_rope[:, :, None, :], (b, s, MLA_HEADS, QK_ROPE_DIM))
    k = jnp.concatenate([k_nope, k_pe], axis=-1)
    q = rms_norm(q, q_norm)
    k = rms_norm(k, k_norm)
    q = jnp.concatenate([q[..., :QK_NOPE_DIM], rope(q[..., QK_NOPE_DIM:], positions)], axis=-1)
    k = jnp.concatenate([k[..., :QK_NOPE_DIM], rope(k[..., QK_NOPE_DIM:], positions)], axis=-1)
    q = q.transpose(0, 2, 1, 3)
    k = k.transpose(0, 2, 1, 3)
    v = v.transpose(0, 2, 1, 3)
    scale = QK_HEAD_DIM ** -0.5
    outs = []
    for i in range(s // ATTN_BLOCK):
        lo = i * ATTN_BLOCK
        hi = lo + ATTN_BLOCK
        qb = q[:, :, lo:hi]
        kb = k[:, :, :hi]
        vb = v[:, :, :hi]
        sc = jnp.einsum('bhqd,bhkd->bhqk', qb, kb).astype(jnp.float32) * scale
        mask = jnp.arange(hi)[None, :] <= (lo + jnp.arange(ATTN_BLOCK))[:, None]
        sc = jnp.where(mask, sc, -jnp.inf)
        p = jax.nn.softmax(sc, axis=-1).astype(v.dtype)
        outs.append(jnp.einsum('bhqk,bhkd->bqhd', p, vb))
    o = jnp.concatenate(outs, axis=1)
    return o.reshape(b, s, MLA_WIDTH)


def causal_depthwise_conv(u, w, bias):
    ch = u.shape[-1]
    y = lax.conv_general_dilated(
        u, w[:, None, :].astype(u.dtype), window_strides=(1,), padding=[(SSD_CONV - 1, 0)],
        dimension_numbers=('NWC', 'WIO', 'NWC'), feature_group_count=ch)
    return y + bias


def segsum(a):
    n = a.shape[-1]
    cs = jnp.cumsum(a, axis=-1)
    diff = cs[..., :, None] - cs[..., None, :]
    mask = jnp.tril(jnp.ones((n, n), dtype=bool))
    return jnp.where(mask, diff, -jnp.inf)


def ssd_chunked(x, dt, a, bmat, cmat):
    b, s, h, p = x.shape
    n = bmat.shape[-1]
    nc = s // SSD_CHUNK
    xd = (x.astype(jnp.float32) * dt[..., None]).reshape(b, nc, SSD_CHUNK, h, p)
    a_dt = (dt * a).reshape(b, nc, SSD_CHUNK, h).transpose(0, 3, 1, 2)
    bc = bmat.astype(jnp.float32).reshape(b, nc, SSD_CHUNK, h, n)
    cc = cmat.astype(jnp.float32).reshape(b, nc, SSD_CHUNK, h, n)
    a_cs = jnp.cumsum(a_dt, axis=-1)
    decay = jnp.exp(segsum(a_dt))
    cb = jnp.einsum('bclhn,bcshn->bhcls', cc, bc) * decay
    y_diag = jnp.einsum('bhcls,bcshp->bclhp', cb, xd)
    decay_states = jnp.exp(a_cs[..., -1:] - a_cs)
    states = jnp.einsum('bclhn,bhcl,bclhp->bchpn', bc, decay_states, xd)
    chunk_decay = jnp.exp(a_cs[..., -1])

    def step(prev, inp):
        st, dec = inp
        return prev * dec[..., None, None] + st, prev

    init = jnp.zeros((b, h, p, n), jnp.float32)
    _, prev_states = lax.scan(step, init, (states.transpose(1, 0, 2, 3, 4), chunk_decay.transpose(2, 0, 1)))
    prev_states = prev_states.transpose(1, 0, 2, 3, 4)
    y_off = jnp.einsum('bclhn,bchpn,bhcl->bclhp', cc, prev_states, jnp.exp(a_cs))
    return (y_diag + y_off).reshape(b, s, h, p)


def ssd_mixer(z, xbc, dt_raw, conv_w, conv_b, dt_bias, a_log, d_skip, ssd_norm):
    b, s, _ = z.shape
    xbc = jax.nn.silu(causal_depthwise_conv(xbc, conv_w, conv_b))
    gn = SSD_GROUPS * SSD_STATE
    xs = xbc[..., :SSD_D_INNER].reshape(b, s, SSD_HEADS, SSD_HEAD_DIM)
    bm = xbc[..., SSD_D_INNER:SSD_D_INNER + gn].reshape(b, s, SSD_GROUPS, SSD_STATE)
    cm = xbc[..., SSD_D_INNER + gn:].reshape(b, s, SSD_GROUPS, SSD_STATE)
    rep = SSD_HEADS // SSD_GROUPS
    bm = jnp.repeat(bm, rep, axis=2)
    cm = jnp.repeat(cm, rep, axis=2)
    dt = jax.nn.softplus(dt_raw.astype(jnp.float32) + dt_bias.astype(jnp.float32))
    a = -jnp.exp(a_log.astype(jnp.float32))
    y = ssd_chunked(xs, dt, a, bm, cm) + xs.astype(jnp.float32) * d_skip.astype(jnp.float32)[:, None]
    y = y.reshape(b, s, SSD_D_INNER) * jax.nn.silu(z.astype(jnp.float32))
    gs = SSD_D_INNER // SSD_GROUPS
    y = rms_norm(y.reshape(b, s, SSD_GROUPS, gs), ssd_norm.reshape(SSD_GROUPS, gs))
    return y.reshape(b, s, SSD_D_INNER).astype(z.dtype)


def peer_ffn(h, w_query, sub_keys, u_experts, v_experts):
    b, s, d = h.shape
    t = b * s
    ht = h.reshape(t, d)
    q = (ht @ w_query).reshape(t, PEER_HEADS, 2, PEER_HALF)
    sc = jnp.einsum('thjd,hjkd->thjk', q, sub_keys).astype(jnp.float32)
    top_s, top_i = lax.top_k(sc, PEER_TOPK)
    cand_s = top_s[:, :, 0, :, None] + top_s[:, :, 1, None, :]
    cand_i = top_i[:, :, 0, :, None] * PEER_KEYS + top_i[:, :, 1, None, :]
    kk = PEER_TOPK * PEER_TOPK
    best_s, best_pos = lax.top_k(cand_s.reshape(t, PEER_HEADS, kk), PEER_TOPK)
    expert_idx = jnp.take_along_axis(cand_i.reshape(t, PEER_HEADS, kk), best_pos, axis=-1)
    gates = jax.nn.softmax(best_s, axis=-1)

    def block(args):
        hb, ib, gb = args
        u = jnp.take(u_experts, ib, axis=0)
        pre = jnp.einsum('thkd,td->thk', u, hb).astype(jnp.float32)
        act = (jax.nn.gelu(pre, approximate=False) * gb).astype(hb.dtype)
        vsel = jnp.take(v_experts, ib, axis=0)
        return jnp.einsum('thk,thkd->td', act, vsel)

    nb = t // PEER_TOKEN_BLOCK
    out = lax.map(block, (ht.reshape(nb, PEER_TOKEN_BLOCK, d),
                          expert_idx.reshape(nb, PEER_TOKEN_BLOCK, PEER_HEADS, PEER_TOPK),
                          gates.reshape(nb, PEER_TOKEN_BLOCK, PEER_HEADS, PEER_TOPK)))
    return out.reshape(b, s, d)


def setup_inputs(seed: int = 0) -> dict:
    key = jax.random.key(seed)
    ks = jax.random.split(key, 24)
    f32 = jnp.float32
    L = DEPTH

    def nrm(k, shape, scale):
        return jax.random.normal(k, shape, f32) * scale

    def gain(k, shape):
        return 1.0 + 0.02 * jax.random.normal(k, shape, f32)

    x = nrm(ks[0], (BATCH, SEQ, D_MODEL), 1.0)
    c = nrm(ks[1], (BATCH, D_MODEL), 1.0)
    offs = jax.random.randint(ks[2], (BATCH, 1), 0, 1024, dtype=jnp.int32)
    positions = jnp.arange(SEQ, dtype=jnp.int32)[None, :] + offs
    w_ada = nrm(ks[3], (L, D_MODEL, N_MOD * D_MODEL), 0.5 * D_MODEL ** -0.5)
    b_ada = nrm(ks[4], (L, N_MOD * D_MODEL), 0.02)
    w_in = nrm(ks[5], (L, D_MODEL, IN_COLS), D_MODEL ** -0.5)
    q_a_norm = gain(ks[6], (L, Q_LORA_RANK))
    w_uq = nrm(ks[7], (L, Q_LORA_RANK, MLA_HEADS * QK_HEAD_DIM), Q_LORA_RANK ** -0.5)
    kv_a_norm = gain(ks[8], (L, KV_LORA_RANK))
    w_ukv = nrm(ks[9], (L, KV_LORA_RANK, MLA_HEADS * (QK_NOPE_DIM + V_HEAD_DIM)), KV_LORA_RANK ** -0.5)
    q_norm = gain(ks[10], (L, QK_HEAD_DIM))
    k_norm = gain(ks[11], (L, QK_HEAD_DIM))
    attn_out_norm = gain(ks[12], (L, MLA_WIDTH))
    conv_w = nrm(ks[13], (L, SSD_CONV, SSD_CONV_DIM), SSD_CONV ** -0.5)
    conv_b = nrm(ks[14], (L, SSD_CONV_DIM), 0.02)
    dt0 = jnp.exp(jax.random.uniform(ks[15], (L, SSD_HEADS), f32, math.log(1e-3), math.log(1e-1)))
    dt_bias = dt0 + jnp.log(-jnp.expm1(-dt0))
    a_log = jnp.log(jax.random.uniform(ks[16], (L, SSD_HEADS), f32, 1.0, 16.0))
    d_skip = gain(ks[17], (L, SSD_HEADS))
    ssd_norm = gain(ks[18], (L, SSD_D_INNER))
    w_out = nrm(ks[19], (L, MIX_WIDTH, D_MODEL), MIX_WIDTH ** -0.5)
    w_query = nrm(ks[20], (L, D_MODEL, PEER_HEADS * PEER_KEY_DIM), D_MODEL ** -0.5)
    sub_keys = nrm(ks[21], (L, PEER_HEADS, 2, PEER_KEYS, PEER_HALF), PEER_HALF ** -0.5)
    u_experts = nrm(ks[22], (L, PEER_EXPERTS, D_MODEL), D_MODEL ** -0.5)
    v_experts = nrm(ks[23], (L, PEER_EXPERTS, D_MODEL), 0.5)
    return {"x": x, "c": c, "positions": positions, "w_ada": w_ada, "b_ada": b_ada, "w_in": w_in,
            "q_a_norm": q_a_norm, "w_uq": w_uq, "kv_a_norm": kv_a_norm, "w_ukv": w_ukv,
            "q_norm": q_norm, "k_norm": k_norm, "attn_out_norm": attn_out_norm,
            "conv_w": conv_w, "conv_b": conv_b, "dt_bias": dt_bias, "a_log": a_log,
            "d_skip": d_skip, "ssd_norm": ssd_norm, "w_out": w_out, "w_query": w_query,
            "sub_keys": sub_keys, "u_experts": u_experts, "v_experts": v_experts}


def reference(x, c, positions, w_ada, b_ada, w_in, q_a_norm, w_uq, kv_a_norm, w_ukv, q_norm, k_norm,
              attn_out_norm, conv_w, conv_b, dt_bias, a_log, d_skip, ssd_norm, w_out, w_query,
              sub_keys, u_experts, v_experts):
    c_act = jax.nn.silu(c)
    for l in range(DEPTH):
        mod = (c_act @ w_ada[l] + b_ada[l])[:, None, :]
        sh1, sc1, g1, sh2, sc2, g2 = jnp.split(mod, N_MOD, axis=-1)
        h = rms_norm(x) * (1.0 + sc1) + sh1
        proj = h @ w_in[l]
        c_q, c_kv, k_rope, z, xbc, dt_raw = jnp.split(proj, IN_SPLITS, axis=-1)
        attn = mla_attention(c_q, c_kv, k_rope, positions, q_a_norm[l], w_uq[l], kv_a_norm[l], w_ukv[l],
                             q_norm[l], k_norm[l])
        attn = rms_norm(attn, attn_out_norm[l])
        ssm = ssd_mixer(z, xbc, dt_raw, conv_w[l], conv_b[l], dt_bias[l], a_log[l], d_skip[l], ssd_norm[l])
        mix = jnp.concatenate([attn, ssm], axis=-1) @ w_out[l]
        x = x + g1 * mix
        h2 = rms_norm(x) * (1.0 + sc2) + sh2
        x = x + g2 * peer_ffn(h2, w_query[l], sub_keys[l], u_experts[l], v_experts[l])
    return x
```

```python
import functools
import math

import jax
import jax.numpy as jnp
from jax import lax
from jax.experimental import pallas as pl
from jax.experimental.pallas import tpu as pltpu

F32 = jnp.float32
BF16 = jnp.bfloat16
NORM_EPS = 1e-6
LANES = 128
SUBLANES = 8
VMEM_LIMIT_BYTES = 56 * 1024 * 1024

N_MOD = 6
MLA_HEADS = 8
QK_NOPE = 128
QK_ROPE = 64
QK_DIM = QK_NOPE + QK_ROPE
QK_PAD = 256
V_DIM = 128
ROPE_THETA = 10000.0
SSD_HEADS = 16
SSD_P = 64
SSD_INNER = SSD_HEADS * SSD_P
SSD_GROUPS = 2
SSD_N = 128
SSD_CONV = 4
SSD_CHUNK = 128
PEER_HEADS = 8
PEER_KEYS = 128
PEER_TOPK = 16
PEER_HALF = 128
NEG_INF = float("-inf")


def _params(*sem):
    return pltpu.CompilerParams(dimension_semantics=sem, vmem_limit_bytes=VMEM_LIMIT_BYTES)


def _dot(a, b):
    return jnp.dot(a, b, preferred_element_type=F32)


def _dot_nt(a, b):
    return lax.dot_general(a, b, (((1,), (1,)), ((), ())), preferred_element_type=F32)


def _dot_tn(a, b):
    return lax.dot_general(a, b, (((0,), (0,)), ((), ())), preferred_element_type=F32)


def _rms(x, width):
    return x * lax.rsqrt(jnp.sum(x * x, axis=-1, keepdims=True) * (1.0 / width) + NORM_EPS)


def _silu(x):
    return x * jax.nn.sigmoid(x)


def _split_bf16(x, pieces):
    out = []
    r = x
    for _ in range(pieces):
        p = r.astype(BF16)
        out.append(p)
        r = r - p.astype(F32)
    return out


def _mod_kernel(c_ref, w_ref, b_ref, o_ref):
    act = _silu(c_ref[...]).astype(BF16)
    o_ref[...] = _dot(act, w_ref[...].astype(BF16)) + b_ref[...]


def _adaln_mod(c, w_ada, b_ada):
    bsz, d = c.shape
    n = w_ada.shape[1]
    tn = 1024
    return pl.pallas_call(
        _mod_kernel,
        grid=(n // tn,),
        in_specs=[pl.BlockSpec((bsz, d), lambda j: (0, 0)),
                  pl.BlockSpec((d, tn), lambda j: (0, j)),
                  pl.BlockSpec((1, tn), lambda j: (0, j))],
        out_specs=pl.BlockSpec((bsz, tn), lambda j: (0, j)),
        out_shape=jax.ShapeDtypeStruct((bsz, n), F32),
        compiler_params=_params("arbitrary"),
        name="adaln_mod",
    )(c, w_ada, b_ada.reshape(1, n))


Q_LORA = 512
KV_LORA = 256
SSD_CONV_DIM = SSD_INNER + 2 * SSD_GROUPS * SSD_N
IN_OFF_CQ = 0
IN_OFF_CKV = IN_OFF_CQ + Q_LORA
IN_OFF_KR = IN_OFF_CKV + KV_LORA
IN_OFF_Z = IN_OFF_KR + LANES
IN_OFF_XBC = IN_OFF_Z + SSD_INNER
IN_OFF_DT = IN_OFF_XBC + SSD_CONV_DIM
IN_PAD_COLS = IN_OFF_DT + LANES


def _in_proj_kernel(x_ref, mod_ref, w_ref, qan_ref, kvan_ref,
                    cq_ref, ckv_ref, kr_ref, z_ref, xbc_ref, dt_ref):
    x = x_ref[...]
    d = x.shape[-1]
    sh1 = mod_ref[0, 0:1, :]
    sc1 = mod_ref[0, 1:2, :]
    h = _rms(x, d) * (1.0 + sc1) + sh1
    proj = _dot(h.astype(BF16), w_ref[...])
    cq = proj[:, IN_OFF_CQ:IN_OFF_CKV]
    cq_ref[...] = (_rms(cq, Q_LORA) * qan_ref[...]).astype(BF16)
    ckv = proj[:, IN_OFF_CKV:IN_OFF_KR]
    ckv_ref[...] = (_rms(ckv, KV_LORA) * kvan_ref[...]).astype(BF16)
    kr_ref[...] = proj[:, IN_OFF_KR:IN_OFF_Z]
    z_ref[...] = proj[:, IN_OFF_Z:IN_OFF_XBC]
    xbc_ref[...] = proj[:, IN_OFF_XBC:IN_OFF_DT]
    dt_ref[...] = proj[:, IN_OFF_DT:IN_PAD_COLS]


def _in_proj(x2d, mod3, w_in_p, q_a_norm, kv_a_norm, seq, tm):
    t, d = x2d.shape
    per_b = seq // tm
    row = lambda i: (i, 0)
    const = lambda i: (0, 0)
    outs = [(Q_LORA, BF16), (KV_LORA, BF16), (LANES, F32), (SSD_INNER, F32), (SSD_CONV_DIM, F32), (LANES, F32)]
    return pl.pallas_call(
        _in_proj_kernel,
        grid=(t // tm,),
        in_specs=[pl.BlockSpec((tm, d), row),
                  pl.BlockSpec((1, N_MOD, d), lambda i: (i // per_b, 0, 0)),
                  pl.BlockSpec((d, IN_PAD_COLS), const),
                  pl.BlockSpec((1, Q_LORA), const),
                  pl.BlockSpec((1, KV_LORA), const)],
        out_specs=[pl.BlockSpec((tm, w), row) for w, _ in outs],
        out_shape=[jax.ShapeDtypeStruct((t, w), dt) for w, dt in outs],
        compiler_params=_params("parallel"),
        name="in_proj",
    )(x2d, mod3, w_in_p, q_a_norm.reshape(1, -1), kv_a_norm.reshape(1, -1))


def _qkv_kernel(cq_ref, ckv_ref, kr_ref, pos_ref, wuq_ref, wuk_ref, wuv_ref,
                gq_ref, gk_ref, invf_ref, q_ref, k_ref, v_ref):
    q = _dot(cq_ref[0], wuq_ref[...])
    kn = _dot(ckv_ref[0], wuk_ref[...])
    vv = _dot(ckv_ref[0], wuv_ref[...])
    kr = kr_ref[0]
    ang = pos_ref[0].astype(F32) * invf_ref[...]
    cos = jnp.cos(ang)
    sin = jnp.sin(ang)
    lane = lax.broadcasted_iota(jnp.int32, ang.shape, 1)
    first = lane < (QK_ROPE // 2)
    sin_signed = jnp.where(first, -sin, sin)

    def rope(xpe):
        rot = jnp.where(first,
                        pltpu.roll(xpe, LANES - QK_ROPE // 2, axis=1),
                        pltpu.roll(xpe, QK_ROPE // 2, axis=1))
        return xpe * cos + rot * sin_signed

    gq_n, gq_p = gq_ref[:, 0:QK_NOPE], gq_ref[:, QK_NOPE:QK_PAD]
    gk_n, gk_p = gk_ref[:, 0:QK_NOPE], gk_ref[:, QK_NOPE:QK_PAD]
    scale = QK_DIM ** -0.5
    kr_ss = jnp.sum(kr * kr, axis=-1, keepdims=True)
    for hd in range(MLA_HEADS):
        qn = q[:, hd * QK_PAD:hd * QK_PAD + QK_NOPE]
        qp = q[:, hd * QK_PAD + QK_NOPE:(hd + 1) * QK_PAD]
        ss = jnp.sum(qn * qn, axis=-1, keepdims=True) + jnp.sum(qp * qp, axis=-1, keepdims=True)
        r = lax.rsqrt(ss * (1.0 / QK_DIM) + NORM_EPS) * scale
        q_ref[0, hd, :, 0:QK_NOPE] = (qn * r * gq_n).astype(BF16)
        q_ref[0, hd, :, QK_NOPE:QK_PAD] = rope(qp * r * gq_p).astype(BF16)
        kh = kn[:, hd * QK_NOPE:(hd + 1) * QK_NOPE]
        ssk = jnp.sum(kh * kh, axis=-1, keepdims=True) + kr_ss
        rk = lax.rsqrt(ssk * (1.0 / QK_DIM) + NORM_EPS)
        k_ref[0, hd, :, 0:QK_NOPE] = (kh * rk * gk_n).astype(BF16)
        k_ref[0, hd, :, QK_NOPE:QK_PAD] = rope(kr * rk * gk_p).astype(BF16)
        v_ref[0, hd] = vv[:, hd * V_DIM:(hd + 1) * V_DIM].astype(BF16)


def _qkv_prep(cq3, ckv3, kr3, pos3, wuq_p, wuk, wuv, gq, gk, invf, tm):
    b, s, _ = cq3.shape
    tok = lambda bi, i: (bi, i, 0)
    const = lambda bi, i: (0, 0)
    head = lambda bi, i: (bi, 0, i, 0)
    return pl.pallas_call(
        _qkv_kernel,
        grid=(b, s // tm),
        in_specs=[pl.BlockSpec((1, tm, Q_LORA), tok),
                  pl.BlockSpec((1, tm, KV_LORA), tok),
                  pl.BlockSpec((1, tm, LANES), tok),
                  pl.BlockSpec((1, tm, 1), tok),
                  pl.BlockSpec(wuq_p.shape, const),
                  pl.BlockSpec(wuk.shape, const),
                  pl.BlockSpec(wuv.shape, const),
                  pl.BlockSpec((1, QK_PAD), const),
                  pl.BlockSpec((1, QK_PAD), const),
                  pl.BlockSpec((1, LANES), const)],
        out_specs=[pl.BlockSpec((1, MLA_HEADS, tm, QK_PAD), head),
                   pl.BlockSpec((1, MLA_HEADS, tm, QK_PAD), head),
                   pl.BlockSpec((1, MLA_HEADS, tm, V_DIM), head)],
        out_shape=[jax.ShapeDtypeStruct((b, MLA_HEADS, s, QK_PAD), BF16),
                   jax.ShapeDtypeStruct((b, MLA_HEADS, s, QK_PAD), BF16),
                   jax.ShapeDtypeStruct((b, MLA_HEADS, s, V_DIM), BF16)],
        compiler_params=_params("parallel", "parallel"),
        name="qkv_prep",
    )(cq3, ckv3, kr3, pos3, wuq_p, wuk, wuv, gq, gk, invf)


def _attn_kernel(q_ref, k_ref, v_ref, o_ref, m_sc, l_sc, acc_sc):
    i = pl.program_id(2)
    j = pl.program_id(3)

    @pl.when(j == 0)
    def _():
        m_sc[...] = jnp.full(m_sc.shape, NEG_INF, F32)
        l_sc[...] = jnp.zeros(l_sc.shape, F32)
        acc_sc[...] = jnp.zeros(acc_sc.shape, F32)

    def step(diagonal):
        s = _dot_nt(q_ref[0, 0], k_ref[0, 0])
        if diagonal:
            row = lax.broadcasted_iota(jnp.int32, s.shape, 0)
            col = lax.broadcasted_iota(jnp.int32, s.shape, 1)
            s = jnp.where(col <= row, s, NEG_INF)
        m_prev = m_sc[...]
        m_new = jnp.maximum(m_prev, jnp.max(s, axis=-1, keepdims=True))
        alpha = jnp.exp(m_prev - m_new)
        p = jnp.exp(s - m_new)
        l_sc[...] = alpha * l_sc[...] + jnp.sum(p, axis=-1, keepdims=True)
        acc_sc[...] = alpha * acc_sc[...] + _dot(p.astype(BF16), v_ref[0, 0])
        m_sc[...] = m_new

    @pl.when(j < i)
    def _():
        step(False)

    @pl.when(j == i)
    def _():
        step(True)
        o_ref[0] = (acc_sc[...] / l_sc[...]).astype(o_ref.dtype)


def _flash_attn(q, k, v, tq):
    b, h, s, _ = q.shape
    n = s // tq
    return pl.pallas_call(
        _attn_kernel,
        grid=(b, h, n, n),
        in_specs=[pl.BlockSpec((1, 1, tq, QK_PAD), lambda bi, hi, i, j: (bi, hi, i, 0)),
                  pl.BlockSpec((1, 1, tq, QK_PAD), lambda bi, hi, i, j: (bi, hi, jnp.minimum(i, j), 0)),
                  pl.BlockSpec((1, 1, tq, V_DIM), lambda bi, hi, i, j: (bi, hi, jnp.minimum(i, j), 0))],
        out_specs=pl.BlockSpec((1, tq, V_DIM), lambda bi, hi, i, j: (bi, i, hi)),
        out_shape=jax.ShapeDtypeStruct((b, s, h * V_DIM), BF16),
        scratch_shapes=[pltpu.VMEM((tq, 1), F32), pltpu.VMEM((tq, 1), F32), pltpu.VMEM((tq, V_DIM), F32)],
        compiler_params=_params("parallel", "parallel", "parallel", "arbitrary"),
        name="flash_attn",
    )(q, k, v)


def _ssd_kernel(z_ref, xbc_ref, dt_ref, cw_ref, cb_ref, dtb_ref, alog_ref, dsk_ref, gn_ref, eh_ref,
                o_ref, ubuf, st):
    c = pl.program_id(1)
    ln = SSD_CHUNK
    halo = SUBLANES

    @pl.when(c == 0)
    def _():
        ubuf[0:halo, :] = jnp.zeros((halo, SSD_CONV_DIM), F32)
        st[...] = jnp.zeros(st.shape, F32)

    @pl.when(c > 0)
    def _():
        ubuf[0:halo, :] = ubuf[ln:ln + halo, :]

    ubuf[halo:halo + ln, :] = xbc_ref[0]
    conv = cb_ref[...]
    for kk in range(SSD_CONV):
        off = halo - (SSD_CONV - 1) + kk
        conv = conv + cw_ref[kk:kk + 1, :] * ubuf[off:off + ln, :]
    u = _silu(conv)

    xdt = dt_ref[0] + dtb_ref[...]
    dt = jnp.maximum(xdt, 0.0) + jnp.log1p(jnp.exp(-jnp.abs(xdt)))
    a = -jnp.exp(alog_ref[...])
    adt = dt * a
    row = lax.broadcasted_iota(jnp.int32, (ln, ln), 0)
    col = lax.broadcasted_iota(jnp.int32, (ln, ln), 1)
    causal = row >= col
    tril = jnp.where(causal, 1.0, 0.0).astype(BF16)
    acs = sum(_dot(tril, p) for p in _split_bf16(adt, 3))
    acs_t = acs.T
    dt_t = dt.T
    last = acs[ln - 1:ln, :]
    eacs = jnp.exp(acs)
    wdec = jnp.exp(last - acs) * dt
    ehead = eh_ref[...]
    e_exp = sum(_dot(p, ehead) for p in _split_bf16(eacs, 2))
    w_exp = sum(_dot(p, ehead) for p in _split_bf16(wdec, 2))
    cdec = e_exp[ln - 1:ln, :]
    xs = u[:, 0:SSD_INNER]
    xw = (xs * w_exp).astype(BF16)
    lane = lax.broadcasted_iota(jnp.int32, (ln, LANES), 1)
    low_half = lane < SSD_P

    gw = SSD_INNER // SSD_GROUPS
    pairs_per_group = gw // LANES
    ys = []
    for g in range(SSD_GROUPS):
        bg = u[:, SSD_INNER + g * SSD_N:SSD_INNER + (g + 1) * SSD_N]
        cg = u[:, SSD_INNER + (SSD_GROUPS + g) * SSD_N:SSD_INNER + (SSD_GROUPS + g + 1) * SSD_N]
        cg16 = cg.astype(BF16)
        cb = _dot_nt(cg16, bg.astype(BF16))
        s_old = st[:, g * gw:(g + 1) * gw]
        y_off = _dot(cg16, s_old.astype(BF16))
        st[:, g * gw:(g + 1) * gw] = (s_old * cdec[:, g * gw:(g + 1) * gw]
                                      + _dot(bg.T.astype(BF16), xw[:, g * gw:(g + 1) * gw]))
        for pr in range(pairs_per_group):
            qi = g * pairs_per_group + pr
            xs_q = xs[:, qi * LANES:(qi + 1) * LANES]
            y_q = jnp.zeros((ln, LANES), F32)
            for sub in range(2):
                hd = 2 * qi + sub
                diff = acs[:, hd:hd + 1] - acs_t[hd:hd + 1, :]
                decay = jnp.exp(jnp.where(causal, diff, NEG_INF))
                m = (cb * decay * dt_t[hd:hd + 1, :]).astype(BF16)
                xm = jnp.where(low_half if sub == 0 else jnp.logical_not(low_half), xs_q, 0.0)
                y_q = y_q + _dot(m, xm.astype(BF16))
            sl = slice(qi * LANES, (qi + 1) * LANES)
            y_q = y_q + y_off[:, pr * LANES:(pr + 1) * LANES] * e_exp[:, sl] + xs_q * dsk_ref[:, sl]
            ys.append(y_q * _silu(z_ref[0, :, sl]))
    for g in range(SSD_GROUPS):
        grp = ys[g * pairs_per_group:(g + 1) * pairs_per_group]
        ss = sum(jnp.sum(y * y, axis=-1, keepdims=True) for y in grp)
        r = lax.rsqrt(ss * (1.0 / gw) + NORM_EPS)
        for pr, y in enumerate(grp):
            sl = slice((g * pairs_per_group + pr) * LANES, (g * pairs_per_group + pr + 1) * LANES)
            o_ref[0, :, sl] = (y * r * gn_ref[:, sl]).astype(o_ref.dtype)


def _ssd(z3, xbc3, dt3, conv_w, conv_b, dtb_p, alog_p, dsk_exp, ssd_norm, ehead):
    b, s, _ = z3.shape
    nc = s // SSD_CHUNK
    tok = lambda bi, ci: (bi, ci, 0)
    const = lambda bi, ci: (0, 0)
    return pl.pallas_call(
        _ssd_kernel,
        grid=(b, nc),
        in_specs=[pl.BlockSpec((1, SSD_CHUNK, SSD_INNER), tok),
                  pl.BlockSpec((1, SSD_CHUNK, SSD_CONV_DIM), tok),
                  pl.BlockSpec((1, SSD_CHUNK, LANES), tok),
                  pl.BlockSpec((SSD_CONV, SSD_CONV_DIM), const),
                  pl.BlockSpec((1, SSD_CONV_DIM), const),
                  pl.BlockSpec((1, LANES), const),
                  pl.BlockSpec((1, LANES), const),
                  pl.BlockSpec((1, SSD_INNER), const),
                  pl.BlockSpec((1, SSD_INNER), const),
                  pl.BlockSpec((LANES, SSD_INNER), const)],
        out_specs=pl.BlockSpec((1, SSD_CHUNK, SSD_INNER), tok),
        out_shape=jax.ShapeDtypeStruct((b, s, SSD_INNER), BF16),
        scratch_shapes=[pltpu.VMEM((SSD_CHUNK + SUBLANES, SSD_CONV_DIM), F32),
                        pltpu.VMEM((SSD_N, SSD_INNER), F32)],
        compiler_params=_params("parallel", "arbitrary"),
        name="ssd_scan",
    )(z3, xbc3, dt3, conv_w, conv_b, dtb_p, alog_p, dsk_exp, ssd_norm, ehead)


def _out_kernel(attn_ref, ssm_ref, x_ref, mod_ref, gan_ref, wa_ref, ws_ref, x1_ref, h2_ref):
    attn = attn_ref[...].astype(F32)
    width = attn.shape[-1]
    attn_n = (_rms(attn, width) * gan_ref[...]).astype(BF16)
    mix = _dot(attn_n, wa_ref[...]) + _dot(ssm_ref[...], ws_ref[...])
    g1 = mod_ref[0, 2:3, :]
    sh2 = mod_ref[0, 3:4, :]
    sc2 = mod_ref[0, 4:5, :]
    x1 = x_ref[...] + g1 * mix
    x1_ref[...] = x1
    h2_ref[...] = (_rms(x1, x1.shape[-1]) * (1.0 + sc2) + sh2).astype(BF16)


def _out_proj(attn2d, ssm2d, x2d, mod3, attn_out_norm, w_out_a, w_out_s, seq, tm):
    t, d = x2d.shape
    wa = attn2d.shape[1]
    ws = ssm2d.shape[1]
    per_b = seq // tm
    row = lambda i: (i, 0)
    const = lambda i: (0, 0)
    return pl.pallas_call(
        _out_kernel,
        grid=(t // tm,),
        in_specs=[pl.BlockSpec((tm, wa), row),
                  pl.BlockSpec((tm, ws), row),
                  pl.BlockSpec((tm, d), row),
                  pl.BlockSpec((1, N_MOD, d), lambda i: (i // per_b, 0, 0)),
                  pl.BlockSpec((1, wa), const),
                  pl.BlockSpec((wa, d), const),
                  pl.BlockSpec((ws, d), const)],
        out_specs=[pl.BlockSpec((tm, d), row), pl.BlockSpec((tm, d), row)],
        out_shape=[jax.ShapeDtypeStruct((t, d), F32), jax.ShapeDtypeStruct((t, d), BF16)],
        compiler_params=_params("parallel"),
        name="out_proj",
    )(attn2d, ssm2d, x2d, mod3, attn_out_norm.reshape(1, -1), w_out_a, w_out_s)


def _scores_kernel(h2_ref, wq_ref, keys_ref, s_ref):
    qp = _dot(h2_ref[...], wq_ref[...]).astype(BF16)
    for hj in range(2 * PEER_HEADS):
        s_ref[hj] = _dot_nt(keys_ref[hj], qp[:, hj * PEER_HALF:(hj + 1) * PEER_HALF])


def _peer_scores(h2, w_query, keys, tm):
    t, d = h2.shape
    nq = w_query.shape[1]
    return pl.pallas_call(
        _scores_kernel,
        grid=(t // tm,),
        in_specs=[pl.BlockSpec((tm, d), lambda i: (i, 0)),
                  pl.BlockSpec((d, nq), lambda i: (0, 0)),
                  pl.BlockSpec(keys.shape, lambda i: (0, 0, 0))],
        out_specs=pl.BlockSpec((2 * PEER_HEADS, PEER_KEYS, tm), lambda i: (0, 0, i)),
        out_shape=jax.ShapeDtypeStruct((2 * PEER_HEADS, PEER_KEYS, t), F32),
        compiler_params=_params("parallel"),
        name="peer_scores",
    )(h2, w_query, keys)


TOPK_TOKENS = SUBLANES * LANES


def _topk_kernel(s_ref, thr_ref, e1_ref, e2_ref, top_sc):
    groups = TOPK_TOKENS // LANES
    for half in range(2):
        for g in range(groups):
            x = s_ref[0, half, :, g * LANES:(g + 1) * LANES]
            for r in range(PEER_TOPK):
                m = jnp.max(x, axis=0, keepdims=True)
                top_sc[half, r, g:g + 1, :] = m
                x = jnp.where(x == m, NEG_INF, x)
    a = [top_sc[0, r] for r in range(PEER_TOPK)]
    b = [top_sc[1, r] for r in range(PEER_TOPK)]
    cands = [a[i] + b[j] for i in range(PEER_TOPK) for j in range(PEER_TOPK)
             if (i + 1) * (j + 1) <= PEER_TOPK]
    tops = []
    for r in range(PEER_TOPK):
        m = functools.reduce(jnp.maximum, cands)
        tops.append(m)
        if r + 1 < PEER_TOPK:
            cands = [jnp.where(cnd == m, NEG_INF, cnd) for cnd in cands]
    tau = tops[-1]
    zsum = functools.reduce(lambda p, q: p + q, [jnp.exp(tp - tops[0]) for tp in tops])
    inv_z = 1.0 / zsum
    for g in range(groups):
        sl = slice(g * LANES, (g + 1) * LANES)
        s1 = s_ref[0, 0, :, sl]
        s2 = s_ref[0, 1, :, sl]
        thr_ref[0, :, sl] = tau[g:g + 1, :] - s1
        e1_ref[0, :, sl] = jnp.exp(s1 - a[0][g:g + 1, :]) * inv_z[g:g + 1, :]
        e2_ref[0, :, sl] = jnp.exp(s2 - b[0][g:g + 1, :])


def _peer_topk(s_t):
    t = s_t.shape[-1]
    s4 = s_t.reshape(PEER_HEADS, 2, PEER_KEYS, t)
    blk = pl.BlockSpec((1, PEER_KEYS, TOPK_TOKENS), lambda tb, hd: (hd, 0, tb))
    shp = jax.ShapeDtypeStruct((PEER_HEADS, PEER_KEYS, t), F32)
    return pl.pallas_call(
        _topk_kernel,
        grid=(t // TOPK_TOKENS, PEER_HEADS),
        in_specs=[pl.BlockSpec((1, 2, PEER_KEYS, TOPK_TOKENS), lambda tb, hd: (hd, 0, 0, tb))],
        out_specs=[blk, blk, blk],
        out_shape=[shp, shp, shp],
        scratch_shapes=[pltpu.VMEM((2, PEER_TOPK, SUBLANES, LANES), F32)],
        compiler_params=_params("parallel", "parallel"),
        name="peer_topk",
    )(s4)


def _peer_kernel(h2_ref, u_ref, v_ref, thr_ref, e1_ref, s_ref, e2_ref, x1_ref, mod_ref, o_ref,
                 acc_sc, act_sc):
    e = pl.program_id(1)

    @pl.when(e == 0)
    def _():
        acc_sc[...] = jnp.zeros(acc_sc.shape, F32)

    pre_t = _dot_nt(u_ref[...], h2_ref[...])
    slabs = u_ref.shape[0] // PEER_KEYS
    for il in range(slabs):
        gate = jnp.zeros((PEER_KEYS, pre_t.shape[1]), F32)
        for hd in range(PEER_HEADS):
            sel = s_ref[hd, 0] >= thr_ref[hd, 0, il:il + 1, :]
            gate = gate + jnp.where(sel, e2_ref[hd] * e1_ref[hd, 0, il:il + 1, :], 0.0)
        p = pre_t[il * PEER_KEYS:(il + 1) * PEER_KEYS, :]
        act = 0.5 * p * (1.0 + lax.erf(p * (2.0 ** -0.5))) * gate
        act_sc[il * PEER_KEYS:(il + 1) * PEER_KEYS, :] = act.astype(BF16)
    acc_sc[...] += _dot_tn(v_ref[...], act_sc[...])

    @pl.when(e == pl.num_programs(1) - 1)
    def _():
        g2 = mod_ref[0, 5:6, :]
        o_ref[...] = x1_ref[...] + g2 * acc_sc[...].T


def _peer_dense(h2, u16, v16, thr, e1, s4, e2, x1, mod3, seq, tm, te):
    t, d = h2.shape
    n_exp = u16.shape[0]
    per_b = seq // tm
    slabs = te // PEER_KEYS
    thr = thr.reshape(PEER_HEADS, PEER_KEYS // slabs, slabs, t)
    e1 = e1.reshape(PEER_HEADS, PEER_KEYS // slabs, slabs, t)
    return pl.pallas_call(
        _peer_kernel,
        grid=(t // tm, n_exp // te),
        in_specs=[pl.BlockSpec((tm, d), lambda i, e: (i, 0)),
                  pl.BlockSpec((te, d), lambda i, e: (e, 0)),
                  pl.BlockSpec((te, d), lambda i, e: (e, 0)),
                  pl.BlockSpec((PEER_HEADS, 1, slabs, tm), lambda i, e: (0, e, 0, i)),
                  pl.BlockSpec((PEER_HEADS, 1, slabs, tm), lambda i, e: (0, e, 0, i)),
                  pl.BlockSpec((PEER_HEADS, 1, PEER_KEYS, tm), lambda i, e: (0, 1, 0, i)),
                  pl.BlockSpec((PEER_HEADS, PEER_KEYS, tm), lambda i, e: (0, 0, i)),
                  pl.BlockSpec((tm, d), lambda i, e: (i, 0)),
                  pl.BlockSpec((1, N_MOD, d), lambda i, e: (i // per_b, 0, 0))],
        out_specs=pl.BlockSpec((tm, d), lambda i, e: (i, 0)),
        out_shape=jax.ShapeDtypeStruct((t, d), F32),
        scratch_shapes=[pltpu.VMEM((d, tm), F32), pltpu.VMEM((te, tm), BF16)],
        compiler_params=_params("parallel", "arbitrary"),
        name="peer_dense",
    )(h2, u16, v16, thr, e1, s4, e2, x1, mod3)


def _pad_cols(w, width):
    return jnp.pad(w, ((0, 0), (0, width - w.shape[1])))


def _layer(x, mod, positions, w_in, q_a_norm, w_uq, kv_a_norm, w_ukv, q_norm, k_norm, attn_out_norm,
           conv_w, conv_b, dt_bias, a_log, d_skip, ssd_norm, w_out, w_query, sub_keys, u_experts, v_experts):
    b, s, d = x.shape
    t = b * s
    x2d = x.reshape(t, d)
    mod3 = mod.reshape(b, N_MOD, d)

    o_cq, o_ckv, o_kr = Q_LORA, Q_LORA + KV_LORA, Q_LORA + KV_LORA + QK_ROPE
    o_z, o_xbc = o_kr + SSD_INNER, o_kr + SSD_INNER + SSD_CONV_DIM
    w_in_p = jnp.concatenate([
        w_in[:, :o_ckv],
        _pad_cols(w_in[:, o_ckv:o_kr], LANES),
        w_in[:, o_kr:o_xbc],
        _pad_cols(w_in[:, o_xbc:], LANES)], axis=1).astype(BF16)
    tm_in = min(256, s)
    cq, ckv, kr, z, xbc, dtr = _in_proj(x2d, mod3, w_in_p, q_a_norm, kv_a_norm, s, tm_in)

    wuq = w_uq.reshape(Q_LORA, MLA_HEADS, QK_DIM)
    wuq_p = jnp.pad(wuq, ((0, 0), (0, 0), (0, QK_PAD - QK_DIM))).reshape(Q_LORA, MLA_HEADS * QK_PAD).astype(BF16)
    wukv = w_ukv.reshape(KV_LORA, MLA_HEADS, QK_NOPE + V_DIM)
    wuk = wukv[:, :, :QK_NOPE].reshape(KV_LORA, MLA_HEADS * QK_NOPE).astype(BF16)
    wuv = wukv[:, :, QK_NOPE:].reshape(KV_LORA, MLA_HEADS * V_DIM).astype(BF16)
    gq = jnp.pad(q_norm, (0, QK_PAD - QK_DIM)).reshape(1, QK_PAD)
    gk = jnp.pad(k_norm, (0, QK_PAD - QK_DIM)).reshape(1, QK_PAD)
    half = QK_ROPE // 2
    inv_freq = 1.0 / (ROPE_THETA ** (jnp.arange(half, dtype=F32) * (2.0 / QK_ROPE)))
    invf = jnp.pad(jnp.concatenate([inv_freq, inv_freq]), (0, LANES - QK_ROPE)).reshape(1, LANES)
    tm_qkv = min(256, s)
    q, k, v = _qkv_prep(cq.reshape(b, s, Q_LORA), ckv.reshape(b, s, KV_LORA), kr.reshape(b, s, LANES),
                        positions.reshape(b, s, 1), wuq_p, wuk, wuv, gq, gk, invf, tm_qkv)
    attn = _flash_attn(q, k, v, min(512, s))

    pad_h = (0, LANES - SSD_HEADS)
    dsk_exp = jnp.repeat(d_skip, SSD_P).reshape(1, SSD_INNER)
    ehead = (jnp.arange(LANES)[:, None] == (jnp.arange(SSD_INNER) // SSD_P)[None, :]).astype(BF16)
    ssm = _ssd(z.reshape(b, s, SSD_INNER), xbc.reshape(b, s, SSD_CONV_DIM), dtr.reshape(b, s, LANES),
               conv_w, conv_b.reshape(1, -1), jnp.pad(dt_bias, pad_h).reshape(1, LANES),
               jnp.pad(a_log, pad_h).reshape(1, LANES), dsk_exp, ssd_norm.reshape(1, -1), ehead)

    wa = MLA_HEADS * V_DIM
    w_out16 = w_out.astype(BF16)
    x1, h2 = _out_proj(attn.reshape(t, wa), ssm.reshape(t, SSD_INNER), x2d, mod3, attn_out_norm,
                       w_out16[:wa], w_out16[wa:], s, min(512, s))

    keys = sub_keys.reshape(2 * PEER_HEADS, PEER_KEYS, PEER_HALF).astype(BF16)
    s_t = _peer_scores(h2, w_query.astype(BF16), keys, min(512, t))
    thr, e1, e2 = _peer_topk(s_t)
    s4 = s_t.reshape(PEER_HEADS, 2, PEER_KEYS, t)
    out = _peer_dense(h2, u_experts.astype(BF16), v_experts.astype(BF16), thr, e1, s4, e2, x1, mod3,
                      s, min(512, s), 512)
    return out.reshape(b, s, d)


def kernel(x, c, positions, w_ada, b_ada, w_in, q_a_norm, w_uq, kv_a_norm, w_ukv, q_norm, k_norm, attn_out_norm, conv_w, conv_b, dt_bias, a_log, d_skip, ssd_norm, w_out, w_query, sub_keys, u_experts, v_experts):
    depth = w_ada.shape[0]
    for l in range(depth):
        mod = _adaln_mod(c, w_ada[l], b_ada[l])
        x = _layer(x, mod, positions, w_in[l], q_a_norm[l], w_uq[l], kv_a_norm[l], w_ukv[l], q_norm[l],
                   k_norm[l], attn_out_norm[l], conv_w[l], conv_b[l], dt_bias[l], a_log[l], d_skip[l],
                   ssd_norm[l], w_out[l], w_query[l], sub_keys[l], u_experts[l], v_experts[l])
    return x
```
